```python
import math
import jax
import jax.numpy as jnp
from jax import lax
import numpy as np

D_MODEL = 1024
BATCH = 2
SEQ = 8192
DEPTH = 1
DEC_BATCH = 128
DEC_SEQ = 4
PAST_LEN = 16384
PAGE_SIZE = 128

DIFF_HEADS = 4
DIFF_QK_DIM = 64
DIFF_V_DIM = 2 * DIFF_QK_DIM
DIFF_WIDTH = DIFF_HEADS * DIFF_V_DIM
DIFF_QK_WIDTH = DIFF_HEADS * 2 * DIFF_QK_DIM
MLA_HEADS = 4
MLA_NOPE_DIM = 64
MLA_ROPE_DIM = 32
MLA_V_DIM = 128
MLA_WIDTH = MLA_HEADS * MLA_V_DIM
Q_LORA_RANK = 384
KV_LORA_RANK = 256
MIX_WIDTH = DIFF_WIDTH + MLA_WIDTH
IN_WIDTH = 2 * DIFF_QK_WIDTH + DIFF_WIDTH + Q_LORA_RANK + KV_LORA_RANK + MLA_ROPE_DIM
IN_OFFSETS = [DIFF_QK_WIDTH, 2 * DIFF_QK_WIDTH, 2 * DIFF_QK_WIDTH + DIFF_WIDTH,
              2 * DIFF_QK_WIDTH + DIFF_WIDTH + Q_LORA_RANK,
              2 * DIFF_QK_WIDTH + DIFF_WIDTH + Q_LORA_RANK + KV_LORA_RANK]
MEM_LEN = 256
MEM_HEADS = 4
MEM_HEAD_DIM = D_MODEL // MEM_HEADS
MEM_WIDTH = MEM_HEADS * MEM_HEAD_DIM
D_FF = ((8 * D_MODEL + 3 * 256 - 1) // (3 * 256)) * 256

QBLOCK = 128
ROPE_BASE = 10000.0
EPS = 1e-6
NEG_INF = -1e30
DIFF_SCALE = DIFF_QK_DIM ** -0.5
MLA_SCALE = (MLA_NOPE_DIM + MLA_ROPE_DIM) ** -0.5
MEM_SCALE = MEM_HEAD_DIM ** -0.5

kernel_name = 'hymba_diff_mla_mem_decoder_step'


def rmsnorm(x, g):
    xf = x.astype(jnp.float32)
    y = xf * lax.rsqrt(jnp.mean(xf * xf, axis=-1, keepdims=True) + EPS)
    return (y * g.astype(jnp.float32)).astype(x.dtype)


def rope(x, pos):
    half = x.shape[-1] // 2
    inv = ROPE_BASE ** (-jnp.arange(half, dtype=jnp.float32) / half)
    ang = pos.astype(jnp.float32)[:, None] * inv[None, :]
    ang = ang.reshape((ang.shape[0],) + (1,) * (x.ndim - 3) + (half,))
    cos, sin = jnp.cos(ang), jnp.sin(ang)
    xf = x.astype(jnp.float32)
    x1, x2 = xf[..., :half], xf[..., half:]
    return jnp.concatenate([x1 * cos - x2 * sin, x1 * sin + x2 * cos], axis=-1).astype(x.dtype)


def alibi_slopes():
    return jnp.asarray(2.0 ** (-8.0 * np.arange(1, DIFF_HEADS + 1) / DIFF_HEADS), dtype=jnp.float32)


def mixer_projections(n, pos, w_in, g_q_lat, w_uq, g_kv_lat):
    B, T, _ = n.shape
    z = n @ w_in
    dq, dk, dv, q_lat, c_kv, k_rope = jnp.split(z, IN_OFFSETS, axis=-1)
    dq = dq.reshape(B, T, DIFF_HEADS, 2 * DIFF_QK_DIM)
    dk = dk.reshape(B, T, DIFF_HEADS, 2 * DIFF_QK_DIM)
    dv = dv.reshape(B, T, DIFF_HEADS, DIFF_V_DIM)
    q = jnp.einsum('btc,che->bthe', rmsnorm(q_lat, g_q_lat), w_uq)
    q_nope = q[..., :MLA_NOPE_DIM]
    q_rope = rope(q[..., MLA_NOPE_DIM:], pos)
    c_kv = rmsnorm(c_kv, g_kv_lat)
    k_rope = rope(k_rope, pos)
    return dq, dk, dv, q_nope, q_rope, c_kv, k_rope


def diff_attend(q, k, v, dist, mask, lam):
    B, Q, H, _ = q.shape
    K = k.shape[1]
    q = q.reshape(B, Q, H, 2, DIFF_QK_DIM)
    k = k.reshape(B, K, H, 2, DIFF_QK_DIM)
    s = jnp.einsum('bqhcd,bkhcd->bhcqk', q, k, preferred_element_type=jnp.float32) * DIFF_SCALE
    s = s - alibi_slopes()[None, :, None, None, None] * dist
    p = jax.nn.softmax(jnp.where(mask, s, NEG_INF), axis=-1)
    w = p[:, :, 0] - lam * p[:, :, 1]
    return jnp.einsum('bhqk,bkhd->bqhd', w.astype(v.dtype), v)


def prompt_attention(dq, dk, dv, q_nope, q_rope, c_kv, k_rope, w_ukv, lam):
    B, S = dq.shape[:2]
    kv = jnp.einsum('bsc,che->bshe', c_kv, w_ukv)
    k_nope, v_mla = kv[..., :MLA_NOPE_DIM], kv[..., MLA_NOPE_DIM:]
    kpos = jnp.arange(S)

    def block(i):
        start = i * QBLOCK
        qpos = start + jnp.arange(QBLOCK)
        mask = kpos[None, :] <= qpos[:, None]
        dist = jnp.abs(qpos[:, None] - kpos[None, :]).astype(jnp.float32)

        def sl(a):
            return lax.dynamic_slice_in_dim(a, start, QBLOCK, axis=1)

        ya = diff_attend(sl(dq), dk, dv, dist, mask, lam)
        s = (jnp.einsum('bqhn,bkhn->bhqk', sl(q_nope), k_nope, preferred_element_type=jnp.float32)
             + jnp.einsum('bqhr,bkr->bhqk', sl(q_rope), k_rope, preferred_element_type=jnp.float32)) * MLA_SCALE
        p = jax.nn.softmax(jnp.where(mask, s, NEG_INF), axis=-1)
        yb = jnp.einsum('bhqk,bkhv->bqhv', p.astype(v_mla.dtype), v_mla)
        return ya, yb

    ya, yb = lax.map(block, jnp.arange(S // QBLOCK))
    ya = jnp.moveaxis(ya, 0, 1).reshape(B, S, DIFF_HEADS, DIFF_V_DIM)
    yb = jnp.moveaxis(yb, 0, 1).reshape(B, S, MLA_HEADS, MLA_V_DIM)
    return ya, yb


def sample_attention(layer, page_table, dq, dk, dv, q_abs, q_rope, c_kv, k_rope,
                     pool_dk, pool_dv, pool_ckv, pool_kr, lam):
    T = dq.shape[1]
    past = page_table.shape[1] * PAGE_SIZE
    qpos = past + jnp.arange(T)
    kpos = jnp.arange(past + T)
    mask = kpos[None, :] <= qpos[:, None]
    dist = jnp.abs(qpos[:, None] - kpos[None, :]).astype(jnp.float32)

    def one(args):
        pg, q1, k1, v1, qa1, qr1, c1, r1 = args
        k_all = jnp.concatenate([pool_dk[layer, pg].reshape((past,) + k1.shape[1:]), k1], axis=0)
        v_all = jnp.concatenate([pool_dv[layer, pg].reshape((past,) + v1.shape[1:]), v1], axis=0)
        ya = diff_attend(q1[None], k_all[None], v_all[None], dist, mask, lam)[0]
        c_all = jnp.concatenate([pool_ckv[layer, pg].reshape(past, KV_LORA_RANK), c1], axis=0)
        r_all = jnp.concatenate([pool_kr[layer, pg].reshape(past, MLA_ROPE_DIM), r1], axis=0)
        s = (jnp.einsum('thc,kc->htk', qa1, c_all, preferred_element_type=jnp.float32)
             + jnp.einsum('thr,kr->htk', qr1, r_all, preferred_element_type=jnp.float32)) * MLA_SCALE
        p = jax.nn.softmax(jnp.where(mask, s, NEG_INF), axis=-1)
        o_lat = jnp.einsum('htk,kc->thc', p.astype(c_all.dtype), c_all)
        return ya, o_lat

    return lax.map(one, (page_table, dq, dk, dv, q_abs, q_rope, c_kv, k_rope))


def mix_out(h, ya, yb, g_subln, lam_init, w_o):
    B, T = h.shape[:2]
    a = rmsnorm(ya, g_subln) * (1.0 - lam_init)
    o = jnp.concatenate([a.reshape(B, T, DIFF_WIDTH), yb.reshape(B, T, MLA_WIDTH)], axis=-1)
    return h + o @ w_o


def cross_attend(n, mem_k, mem_v, w_mq, w_mo):
    B, T, _ = n.shape
    q = (n @ w_mq).reshape(B, T, MEM_HEADS, MEM_HEAD_DIM)
    s = jnp.einsum('bthd,bmhd->bhtm', q, mem_k, preferred_element_type=jnp.float32) * MEM_SCALE
    p = jax.nn.softmax(s, axis=-1)
    o = jnp.einsum('bhtm,bmhd->bthd', p.astype(mem_v.dtype), mem_v).reshape(B, T, MEM_WIDTH)
    return o @ w_mo


def swiglu(n, w_gate, w_up, w_down):
    return (jax.nn.silu(n @ w_gate) * (n @ w_up)) @ w_down


def setup_inputs(seed: int = 0) -> dict:
    key = jax.random.key(seed)
    ks = iter(jax.random.split(key, 48))
    f32 = jnp.float32

    def nrm(shape, scale=1.0):
        return jax.random.normal(next(ks), shape, f32) * scale

    def gain(shape):
        return 1.0 + nrm(shape, 0.02)

    n_pages = PAST_LEN // PAGE_SIZE
    n_used = DEC_BATCH * n_pages
    n_pool = n_used + (n_used + 3) // 4
    L = DEPTH
    perm = jax.random.permutation(next(ks), n_pool)[:n_used]
    page_table = perm.reshape(DEC_BATCH, n_pages).astype(jnp.int32)
    return {
        'x_prompt': nrm((BATCH, SEQ, D_MODEL)),
        'x_sample': nrm((DEC_BATCH, DEC_SEQ, D_MODEL)),
        'mem_prompt': nrm((BATCH, MEM_LEN, D_MODEL)),
        'cache_diff_k': nrm((L, n_pool, PAGE_SIZE, DIFF_HEADS, 2 * DIFF_QK_DIM)),
        'cache_diff_v': nrm((L, n_pool, PAGE_SIZE, DIFF_HEADS, DIFF_V_DIM)),
        'cache_mla_ckv': nrm((L, n_pool, PAGE_SIZE, KV_LORA_RANK)),
        'cache_mla_krope': nrm((L, n_pool, PAGE_SIZE, MLA_ROPE_DIM)),
        'cache_mem_k': nrm((L, DEC_BATCH, MEM_LEN, MEM_HEADS, MEM_HEAD_DIM)),
        'cache_mem_v': nrm((L, DEC_BATCH, MEM_LEN, MEM_HEADS, MEM_HEAD_DIM)),
        'page_table': page_table,
        'g_attn': gain((L, D_MODEL)),
        'w_in': nrm((L, D_MODEL, IN_WIDTH), D_MODEL ** -0.5),
        'lambda_q1': nrm((L, DIFF_QK_DIM), 0.1),
        'lambda_k1': nrm((L, DIFF_QK_DIM), 0.1),
        'lambda_q2': nrm((L, DIFF_QK_DIM), 0.1),
        'lambda_k2': nrm((L, DIFF_QK_DIM), 0.1),
        'g_subln': gain((L, DIFF_V_DIM)),
        'g_q_lat': gain((L, Q_LORA_RANK)),
        'w_uq': nrm((L, Q_LORA_RANK, MLA_HEADS, MLA_NOPE_DIM + MLA_ROPE_DIM), Q_LORA_RANK ** -0.5),
        'g_kv_lat': gain((L, KV_LORA_RANK)),
        'w_ukv': nrm((L, KV_LORA_RANK, MLA_HEADS, MLA_NOPE_DIM + MLA_V_DIM), KV_LORA_RANK ** -0.5),
        'w_o': nrm((L, MIX_WIDTH, D_MODEL), MIX_WIDTH ** -0.5),
        'g_cross': gain((L, D_MODEL)),
        'g_mem': gain((L, D_MODEL)),
        'w_mq': nrm((L, D_MODEL, MEM_WIDTH), D_MODEL ** -0.5),
        'w_mk': nrm((L, D_MODEL, MEM_WIDTH), D_MODEL ** -0.5),
        'w_mv': nrm((L, D_MODEL, MEM_WIDTH), D_MODEL ** -0.5),
        'w_mo': nrm((L, MEM_WIDTH, D_MODEL), MEM_WIDTH ** -0.5),
        'g_ffn': gain((L, D_MODEL)),
        'w_gate': nrm((L, D_MODEL, D_FF), D_MODEL ** -0.5),
        'w_up': nrm((L, D_MODEL, D_FF), D_MODEL ** -0.5),
        'w_down': nrm((L, D_FF, D_MODEL), D_FF ** -0.5),
        'g_final': gain((D_MODEL,)),
    }


def reference(x_prompt, x_sample, mem_prompt, cache_diff_k, cache_diff_v, cache_mla_ckv,
              cache_mla_krope, cache_mem_k, cache_mem_v, page_table, g_attn, w_in,
              lambda_q1, lambda_k1, lambda_q2, lambda_k2, g_subln, g_q_lat, w_uq, g_kv_lat,
              w_ukv, w_o, g_cross, g_mem, w_mq, w_mk, w_mv, w_mo, g_ffn, w_gate, w_up,
              w_down, g_final):
    B, S, _ = x_prompt.shape
    T = x_sample.shape[1]
    past = page_table.shape[1] * PAGE_SIZE
    pos_p = jnp.arange(S)
    pos_s = past + jnp.arange(T)
    hp, hs = x_prompt, x_sample
    p_dk, p_dv, p_ckv, p_kr, p_mk, p_mv = [], [], [], [], [], []
    s_dk, s_dv, s_ckv, s_kr = [], [], [], []
    for l in range(DEPTH):
        lam_init = 0.8 - 0.6 * math.exp(-0.3 * l)
        lam = (jnp.exp(jnp.sum(lambda_q1[l].astype(jnp.float32) * lambda_k1[l].astype(jnp.float32)))
               - jnp.exp(jnp.sum(lambda_q2[l].astype(jnp.float32) * lambda_k2[l].astype(jnp.float32)))
               + lam_init)
        w_uk = w_ukv[l][..., :MLA_NOPE_DIM]
        w_uv = w_ukv[l][..., MLA_NOPE_DIM:]

        dq, dk, dv, qn, qr, ckv, kr = mixer_projections(
            rmsnorm(hp, g_attn[l]), pos_p, w_in[l], g_q_lat[l], w_uq[l], g_kv_lat[l])
        ya, yb = prompt_attention(dq, dk, dv, qn, qr, ckv, kr, w_ukv[l], lam)
        hp = mix_out(hp, ya, yb, g_subln[l], lam_init, w_o[l])
        mem_n = rmsnorm(mem_prompt, g_mem[l])
        mk = (mem_n @ w_mk[l]).reshape(B, MEM_LEN, MEM_HEADS, MEM_HEAD_DIM)
        mv = (mem_n @ w_mv[l]).reshape(B, MEM_LEN, MEM_HEADS, MEM_HEAD_DIM)
        hp = hp + cross_attend(rmsnorm(hp, g_cross[l]), mk, mv, w_mq[l], w_mo[l])
        hp = hp + swiglu(rmsnorm(hp, g_ffn[l]), w_gate[l], w_up[l], w_down[l])
        p_dk.append(dk)
        p_dv.append(dv)
        p_ckv.append(ckv)
        p_kr.append(kr)
        p_mk.append(mk)
        p_mv.append(mv)

        dq_s, dk_s, dv_s, qn_s, qr_s, ckv_s, kr_s = mixer_projections(
            rmsnorm(hs, g_attn[l]), pos_s, w_in[l], g_q_lat[l], w_uq[l], g_kv_lat[l])
        q_abs = jnp.einsum('bthn,chn->bthc', qn_s, w_uk)
        ya_s, olat_s = sample_attention(l, page_table, dq_s, dk_s, dv_s, q_abs, qr_s, ckv_s, kr_s,
                                        cache_diff_k, cache_diff_v, cache_mla_ckv, cache_mla_krope, lam)
        yb_s = jnp.einsum('bthc,chv->bthv', olat_s, w_uv)
        hs = mix_out(hs, ya_s, yb_s, g_subln[l], lam_init, w_o[l])
        hs = hs + cross_attend(rmsnorm(hs, g_cross[l]), cache_mem_k[l], cache_mem_v[l], w_mq[l], w_mo[l])
        hs = hs + swiglu(rmsnorm(hs, g_ffn[l]), w_gate[l], w_up[l], w_down[l])
        s_dk.append(dk_s)
        s_dv.append(dv_s)
        s_ckv.append(ckv_s)
        s_kr.append(kr_s)

    y_prompt = rmsnorm(hp, g_final)
    y_sample = rmsnorm(hs, g_final)
    return (y_prompt, y_sample,
            jnp.stack(p_dk), jnp.stack(p_dv), jnp.stack(p_ckv), jnp.stack(p_kr),
            jnp.stack(p_mk), jnp.stack(p_mv),
            jnp.stack(s_dk), jnp.stack(s_dv), jnp.stack(s_ckv), jnp.stack(s_kr))
```

```python
import functools
import math

import jax
import jax.numpy as jnp
import numpy as np
from jax import lax
from jax.experimental import pallas as pl
from jax.experimental.pallas import tpu as pltpu

F32 = jnp.float32
BF16 = jnp.bfloat16

D_MODEL = 1024
PAGE = 128
N_HEADS = 4
HEAD_W = 128
DIFF_QK = 64
Q_LORA = 384
KV_LORA = 256
NOPE = 64
ROPE = 32
MEM_HEAD = 256
ROPE_BASE = 10000.0
EPS = 1e-6
NEG_INF = -1e30
DIFF_SCALE = DIFF_QK ** -0.5
MLA_SCALE = (NOPE + ROPE) ** -0.5
MEM_SCALE = MEM_HEAD ** -0.5
LAM_INIT = 0.8 - 0.6 * math.exp(-0.3 * 0)
ALIBI = tuple(float(2.0 ** (-8.0 * (h + 1) / N_HEADS)) for h in range(N_HEADS))

HW = N_HEADS * HEAD_W
OFF_DQ, OFF_DK, OFF_DV, OFF_QLAT, OFF_CKV, OFF_KR, OFF_KRS, IN_EXT = (
    0, HW, 2 * HW, 3 * HW, 3 * HW + Q_LORA, 3 * HW + Q_LORA + KV_LORA,
    3 * HW + Q_LORA + KV_LORA + HEAD_W, 3 * HW + Q_LORA + KV_LORA + 2 * HEAD_W)

VMEM_LIMIT = 56 * 1024 * 1024
NT = (((1,), (1,)), ((), ()))


def _rms(x, g):
    return x * lax.rsqrt(jnp.mean(x * x, axis=-1, keepdims=True) + EPS) * g


def _dot(a, b):
    return jnp.dot(a, b, preferred_element_type=F32)


def _dot_nt(a, b):
    return lax.dot_general(a, b, NT, preferred_element_type=F32)


def _lam(lq1, lk1, lq2, lk2):
    return (jnp.exp(jnp.sum(lq1 * lk1, axis=-1, keepdims=True))
            - jnp.exp(jnp.sum(lq2 * lk2, axis=-1, keepdims=True)) + LAM_INIT)


def _full(shape):
    nd = len(shape)
    return pl.BlockSpec(shape, lambda *_: (0,) * nd)


def _params(sem):
    return pltpu.CompilerParams(dimension_semantics=sem, vmem_limit_bytes=VMEM_LIMIT)


def _proj_common(x_ref, cs_ref, sn_ref, g_attn_ref, w_in_ref, gq_ref, wuq_ref, wuqs_ref, gkv_ref):
    n = _rms(x_ref[...], g_attn_ref[...]).astype(BF16)

    def mm(lo, hi):
        return _dot(n, w_in_ref[:, lo:hi])

    cs, sn = cs_ref[...], sn_ref[...]
    nq = _rms(mm(OFF_QLAT, OFF_CKV), gq_ref[...]).astype(BF16)
    q = _dot(nq, wuq_ref[...])
    qs = _dot(nq, wuqs_ref[...])
    ckv = _rms(mm(OFF_CKV, OFF_KR), gkv_ref[...])
    kr = mm(OFF_KR, OFF_KRS) * cs + mm(OFF_KRS, IN_EXT) * sn
    return mm, cs, sn, q, qs, ckv, kr


def _proj_prompt_kernel(x_ref, cs_ref, sn_ref, g_attn_ref, w_in_ref, gq_ref, wuq_ref, wuqs_ref,
                        gkv_ref, wuk_ref, wuv_ref,
                        dk_ref, dv_ref, ckv_ref, kr_ref, qa_ref, qb_ref, kd_ref, vd_ref,
                        qm_ref, km_ref, vm_ref):
    mm, cs, sn, q, qs, ckv, kr = _proj_common(
        x_ref, cs_ref, sn_ref, g_attn_ref, w_in_ref, gq_ref, wuq_ref, wuqs_ref, gkv_ref)
    dq = mm(OFF_DQ, OFF_DK) * DIFF_SCALE
    first = (lax.broadcasted_iota(jnp.int32, dq.shape, 1) % HEAD_W) < DIFF_QK
    qa_ref[...] = jnp.where(first, dq, 0.0).astype(BF16)
    qb_ref[...] = jnp.where(first, 0.0, dq).astype(BF16)
    dk = mm(OFF_DK, OFF_DV)
    dk_ref[...] = dk
    kd_ref[...] = dk.astype(BF16)
    dv = mm(OFF_DV, OFF_QLAT)
    dv_ref[...] = dv
    vd_ref[...] = dv.astype(BF16)
    ckv_ref[...] = ckv
    kr_ref[...] = kr[:, NOPE:NOPE + ROPE]
    cb = ckv.astype(BF16)
    kn = _dot(cb, wuk_ref[...])
    vm_ref[...] = _dot(cb, wuv_ref[...]).astype(BF16)
    for h in range(N_HEADS):
        sl = slice(h * HEAD_W, (h + 1) * HEAD_W)
        qm_ref[:, sl] = ((q[:, sl] * cs + qs[:, sl] * sn) * MLA_SCALE).astype(BF16)
        km_ref[:, sl] = (kn[:, sl] + kr).astype(BF16)


def _proj_sample_kernel(x_ref, cs_ref, sn_ref, g_attn_ref, w_in_ref, gq_ref, wuq_ref, wuqs_ref,
                        gkv_ref, wabs_ref,
                        dk_ref, dv_ref, ckv_ref, kr_ref, dq_ref, qrot_ref, qabs_ref, krw_ref):
    mm, cs, sn, q, qs, ckv, kr = _proj_common(
        x_ref, cs_ref, sn_ref, g_attn_ref, w_in_ref, gq_ref, wuq_ref, wuqs_ref, gkv_ref)
    dq_ref[...] = mm(OFF_DQ, OFF_DK) * DIFF_SCALE
    dk_ref[...] = mm(OFF_DK, OFF_DV)
    dv_ref[...] = mm(OFF_DV, OFF_QLAT)
    ckv_ref[...] = ckv
    kr_ref[...] = kr[:, NOPE:NOPE + ROPE]
    krw_ref[...] = kr
    for h in range(N_HEADS):
        sl = slice(h * HEAD_W, (h + 1) * HEAD_W)
        qrot_ref[:, sl] = (q[:, sl] * cs + qs[:, sl] * sn) * MLA_SCALE
        qabs_ref[:, h * KV_LORA:(h + 1) * KV_LORA] = (
            _dot(q[:, sl].astype(BF16), wabs_ref[h]) * MLA_SCALE)


def _rope_tables(pos):
    half = ROPE // 2
    inv = ROPE_BASE ** (-jnp.arange(half, dtype=F32) / half)
    ang = pos.astype(F32)[:, None] * inv[None, :]
    cos, sin = jnp.cos(ang), jnp.sin(ang)
    n = pos.shape[0]
    cs = jnp.concatenate([jnp.ones((n, NOPE), F32), cos, cos, jnp.ones((n, HEAD_W - NOPE - ROPE), F32)], axis=1)
    sn = jnp.concatenate([jnp.zeros((n, NOPE), F32), -sin, sin, jnp.zeros((n, HEAD_W - NOPE - ROPE), F32)], axis=1)
    return cs, sn


def _prep_proj_weights(w_in, w_uq, w_ukv):
    half = ROPE // 2
    wkr = w_in[:, OFF_KR:OFF_KR + ROPE]
    pad_l = jnp.zeros((D_MODEL, NOPE), F32)
    pad_r = jnp.zeros((D_MODEL, HEAD_W - NOPE - ROPE), F32)
    wkr_sw = jnp.concatenate([wkr[:, half:], wkr[:, :half]], axis=1)
    w_in_ext = jnp.concatenate(
        [w_in[:, :OFF_KR], pad_l, wkr, pad_r, pad_l, wkr_sw, pad_r], axis=1).astype(BF16)
    zq = jnp.zeros((Q_LORA, N_HEADS, HEAD_W - NOPE - ROPE), F32)
    wuq = jnp.concatenate([w_uq, zq], axis=-1).reshape(Q_LORA, HW).astype(BF16)
    wuq_sw = jnp.concatenate(
        [jnp.zeros((Q_LORA, N_HEADS, NOPE), F32), w_uq[..., NOPE + half:], w_uq[..., NOPE:NOPE + half], zq],
        axis=-1).reshape(Q_LORA, HW).astype(BF16)
    w_uk = w_ukv[..., :NOPE]
    w_uv = w_ukv[..., NOPE:]
    wuk = jnp.concatenate([w_uk, jnp.zeros((KV_LORA, N_HEADS, HEAD_W - NOPE), F32)], axis=-1)
    wuk = wuk.reshape(KV_LORA, HW).astype(BF16)
    wuv = w_uv.reshape(KV_LORA, HW).astype(BF16)
    wabs = jnp.transpose(w_uk, (1, 2, 0))
    wabs = jnp.concatenate([wabs, jnp.zeros((N_HEADS, HEAD_W - NOPE, KV_LORA), F32)], axis=1).astype(BF16)
    return w_in_ext, wuq, wuq_sw, wuk, wuv, wabs, w_uv


def _row_tile(n, want):
    t = min(n, want)
    assert n % t == 0, (n, t)
    return t


def _proj_prompt(x2, cs, sn, g_attn, w_in_ext, gq, wuq, wuqs, gkv, wuk, wuv, seq):
    m = x2.shape[0]
    tm = _row_tile(seq, 512)
    per_seq = seq // tm
    row = lambda w: pl.BlockSpec((tm, w), lambda i: (i, 0))
    tab = pl.BlockSpec((tm, HEAD_W), lambda i: (i % per_seq, 0))
    f = lambda w: jax.ShapeDtypeStruct((m, w), F32)
    b = lambda w: jax.ShapeDtypeStruct((m, w), BF16)
    return pl.pallas_call(
        _proj_prompt_kernel,
        grid=(m // tm,),
        in_specs=[row(D_MODEL), tab, tab, _full(g_attn.shape), _full(w_in_ext.shape), _full(gq.shape),
                  _full(wuq.shape), _full(wuqs.shape), _full(gkv.shape), _full(wuk.shape), _full(wuv.shape)],
        out_specs=[row(HW), row(HW), row(KV_LORA), row(ROPE)] + [row(HW)] * 7,
        out_shape=[f(HW), f(HW), f(KV_LORA), f(ROPE)] + [b(HW)] * 7,
        compiler_params=_params(("parallel",)),
        name="proj_prompt",
    )(x2, cs, sn, g_attn, w_in_ext, gq, wuq, wuqs, gkv, wuk, wuv)


def _proj_sample(x2, cs, sn, g_attn, w_in_ext, gq, wuq, wuqs, gkv, wabs):
    m = x2.shape[0]
    tm = _row_tile(m, 256)
    row = lambda w: pl.BlockSpec((tm, w), lambda i: (i, 0))
    f = lambda w: jax.ShapeDtypeStruct((m, w), F32)
    return pl.pallas_call(
        _proj_sample_kernel,
        grid=(m // tm,),
        in_specs=[row(D_MODEL), row(HEAD_W), row(HEAD_W), _full(g_attn.shape), _full(w_in_ext.shape),
                  _full(gq.shape), _full(wuq.shape), _full(wuqs.shape), _full(gkv.shape), _full(wabs.shape)],
        out_specs=[row(HW), row(HW), row(KV_LORA), row(ROPE), row(HW), row(HW), row(N_HEADS * KV_LORA),
                   row(HEAD_W)],
        out_shape=[f(HW), f(HW), f(KV_LORA), f(ROPE), f(HW), f(HW), f(N_HEADS * KV_LORA), f(HEAD_W)],
        compiler_params=_params(("parallel",)),
        name="proj_sample",
    )(x2, cs, sn, g_attn, w_in_ext, gq, wuq, wuqs, gkv, wabs)


def _online_update(s, m_ref, l_ref, acc_ref, idx, v):
    m_prev = m_ref[idx]
    m_new = jnp.maximum(m_prev, jnp.max(s, axis=-1, keepdims=True))
    alpha = jnp.exp(m_prev - m_new)
    p = jnp.exp(s - m_new)
    l_ref[idx] = alpha * l_ref[idx] + jnp.sum(p, axis=-1, keepdims=True)
    acc_ref[idx] = alpha * acc_ref[idx] + _dot(p.astype(BF16), v)
    m_ref[idx] = m_new


def _prompt_attn_kernel(lq1_ref, lk1_ref, lq2_ref, lk2_ref, qa_ref, qb_ref, qm_ref,
                        kd_ref, vd_ref, km_ref, vm_ref, ya_ref, yb_ref, m_ref, l_ref, acc_ref, *, tq, tk):
    qi, ki = pl.program_id(1), pl.program_id(2)

    @pl.when(ki == 0)
    def _():
        m_ref[...] = jnp.full(m_ref.shape, NEG_INF, F32)
        l_ref[...] = jnp.zeros(l_ref.shape, F32)
        acc_ref[...] = jnp.zeros(acc_ref.shape, F32)

    @pl.when(ki * tk < (qi + 1) * tq)
    def _():
        rel = (qi * tq - ki * tk
               + lax.broadcasted_iota(jnp.int32, (tq, tk), 0) - lax.broadcasted_iota(jnp.int32, (tq, tk), 1))
        mask = rel >= 0
        dist = rel.astype(F32)
        for h in range(N_HEADS):
            sl = slice(h * HEAD_W, (h + 1) * HEAD_W)
            k = kd_ref[0, :, sl]
            v = vd_ref[0, :, sl]
            bias = ALIBI[h] * dist
            for c, q_ref in enumerate((qa_ref, qb_ref)):
                s = _dot_nt(q_ref[0, :, sl], k) - bias
                _online_update(jnp.where(mask, s, NEG_INF), m_ref, l_ref, acc_ref, 2 * h + c, v)
            s = _dot_nt(qm_ref[0, :, sl], km_ref[0, :, sl])
            _online_update(jnp.where(mask, s, NEG_INF), m_ref, l_ref, acc_ref, 2 * N_HEADS + h,
                           vm_ref[0, :, sl])

    @pl.when(((ki + 1) * tk >= (qi + 1) * tq) & (ki * tk < (qi + 1) * tq))
    def _():
        lam = _lam(lq1_ref[...], lk1_ref[...], lq2_ref[...], lk2_ref[...])
        for h in range(N_HEADS):
            sl = slice(h * HEAD_W, (h + 1) * HEAD_W)
            ya_ref[0, :, sl] = (acc_ref[2 * h] / l_ref[2 * h]
                                - lam * (acc_ref[2 * h + 1] / l_ref[2 * h + 1]))
            yb_ref[0, :, sl] = acc_ref[2 * N_HEADS + h] / l_ref[2 * N_HEADS + h]


def _prompt_attention(lams, qa, qb, qm, kd, vd, km, vm):
    bsz, seq, _ = qa.shape
    tq = _row_tile(seq, 512)
    tk = tq
    qspec = pl.BlockSpec((1, tq, HW), lambda b, i, j: (b, i, 0))
    kspec = pl.BlockSpec((1, tk, HW), lambda b, i, j: (b, jnp.minimum(j, ((i + 1) * tq - 1) // tk), 0))
    n_state = 3 * N_HEADS
    return pl.pallas_call(
        functools.partial(_prompt_attn_kernel, tq=tq, tk=tk),
        grid=(bsz, seq // tq, seq // tk),
        in_specs=[_full(l.shape) for l in lams] + [qspec] * 3 + [kspec] * 4,
        out_specs=[qspec, qspec],
        out_shape=[jax.ShapeDtypeStruct((bsz, seq, HW), F32)] * 2,
        scratch_shapes=[pltpu.VMEM((n_state, tq, 1), F32), pltpu.VMEM((n_state, tq, 1), F32),
                        pltpu.VMEM((n_state, tq, HEAD_W), F32)],
        compiler_params=_params(("parallel", "parallel", "arbitrary")),
        name="prompt_attention",
    )(*lams, qa, qb, qm, kd, vd, km, vm)


def _sample_attn_kernel(pt_ref, lq1_ref, lk1_ref, lq2_ref, lk2_ref, qd_ref, qabs_ref, qr_ref,
                        kn_ref, vn_ref, cn_ref, rn_ref, *rest, n_tok, pages_per_step, past):
    del pt_ref
    npg = pages_per_step
    dk_refs, dv_refs = rest[:npg], rest[npg:2 * npg]
    ck_refs, kr_refs = rest[2 * npg:3 * npg], rest[3 * npg:4 * npg]
    ya_ref, ol_ref, md_ref, ld_ref, ad_ref, mm_ref, lm_ref, am_ref = rest[4 * npg:]
    c = pl.program_id(1)
    rows_d = 2 * n_tok
    rows_m = N_HEADS * n_tok

    @pl.when(c == 0)
    def _():
        def tok_bias(rows):
            t = lax.broadcasted_iota(jnp.int32, (rows, 8), 0) % n_tok
            j = lax.broadcasted_iota(jnp.int32, (rows, 8), 1)
            return (j <= t) & (j < n_tok), (t - j).astype(F32)

        mask, dist = tok_bias(rows_d)
        for h in range(N_HEADS):
            s = _dot_nt(qd_ref[0, h], kn_ref[0, h]) - ALIBI[h] * dist
            s = jnp.where(mask, s, NEG_INF)
            m = jnp.max(s, axis=-1, keepdims=True)
            p = jnp.exp(s - m)
            md_ref[h] = m
            ld_ref[h] = jnp.sum(p, axis=-1, keepdims=True)
            ad_ref[h] = _dot(p.astype(BF16), vn_ref[0, h])
        mask, _ = tok_bias(rows_m)
        cn = cn_ref[0]
        s = jnp.where(mask, _dot_nt(qabs_ref[0], cn) + _dot_nt(qr_ref[0], rn_ref[0]), NEG_INF)
        m = jnp.max(s, axis=-1, keepdims=True)
        p = jnp.exp(s - m)
        mm_ref[...] = m
        lm_ref[...] = jnp.sum(p, axis=-1, keepdims=True)
        am_ref[...] = _dot(p.astype(BF16), cn)

    width = npg * PAGE
    kpos = c * width + lax.broadcasted_iota(jnp.int32, (rows_d, width), 1)
    qpos = past + lax.broadcasted_iota(jnp.int32, (rows_d, width), 0) % n_tok
    dist = (qpos - kpos).astype(F32)
    for h in range(N_HEADS):
        qh = qd_ref[0, h]
        s = jnp.concatenate(
            [_dot_nt(qh, dk_refs[i][0, pl.ds(h, PAGE, stride=N_HEADS), :].astype(BF16)) for i in range(npg)],
            axis=1) - ALIBI[h] * dist
        m_prev = md_ref[h]
        m_new = jnp.maximum(m_prev, jnp.max(s, axis=-1, keepdims=True))
        alpha = jnp.exp(m_prev - m_new)
        p = jnp.exp(s - m_new)
        ld_ref[h] = alpha * ld_ref[h] + jnp.sum(p, axis=-1, keepdims=True)
        pb = p.astype(BF16)
        acc = alpha * ad_ref[h]
        for i in range(npg):
            acc += _dot(pb[:, i * PAGE:(i + 1) * PAGE],
                        dv_refs[i][0, pl.ds(h, PAGE, stride=N_HEADS), :].astype(BF16))
        ad_ref[h] = acc
        md_ref[h] = m_new

    qabs, qr = qabs_ref[0], qr_ref[0]
    cks = [ck_refs[i][0].astype(BF16) for i in range(npg)]
    s = jnp.concatenate(
        [_dot_nt(qabs, cks[i]) + _dot_nt(qr, kr_refs[i][0].astype(BF16)) for i in range(npg)], axis=1)
    m_prev = mm_ref[...]
    m_new = jnp.maximum(m_prev, jnp.max(s, axis=-1, keepdims=True))
    alpha = jnp.exp(m_prev - m_new)
    p = jnp.exp(s - m_new)
    lm_ref[...] = alpha * lm_ref[...] + jnp.sum(p, axis=-1, keepdims=True)
    pb = p.astype(BF16)
    acc = alpha * am_ref[...]
    for i in range(npg):
        acc += _dot(pb[:, i * PAGE:(i + 1) * PAGE], cks[i])
    am_ref[...] = acc
    mm_ref[...] = m_new

    @pl.when(c == pl.num_programs(1) - 1)
    def _():
        lam = _lam(lq1_ref[...], lk1_ref[...], lq2_ref[...], lk2_ref[...])
        for h in range(N_HEADS):
            o = ad_ref[h] / ld_ref[h]
            ya_ref[0, h] = o[:n_tok] - lam * o[n_tok:]
        ol_ref[0] = am_ref[...] / lm_ref[...]


def _sample_attention(page_table, lams, qd, qabs, qr, kn, vn, cn, rn, pool_dk, pool_dv, pool_ckv, pool_kr,
                      n_tok):
    nb, n_pages = page_table.shape
    npg = 8 if n_pages % 8 == 0 else 1
    past = n_pages * PAGE
    rows_d, rows_m = 2 * n_tok, N_HEADS * n_tok

    def per_seq(shape):
        nd = len(shape)
        return pl.BlockSpec((1,) + shape, lambda b, c, pt: (b,) + (0,) * nd)

    def page(shape, i):
        return pl.BlockSpec((1,) + shape, lambda b, c, pt: (pt[b, c * npg + i], 0, 0))

    in_specs = ([pl.BlockSpec(l.shape, lambda b, c, pt: (0, 0)) for l in lams]
                + [per_seq((N_HEADS, rows_d, HEAD_W)), per_seq((rows_m, KV_LORA)), per_seq((rows_m, ROPE)),
                   per_seq((N_HEADS, 8, HEAD_W)), per_seq((N_HEADS, 8, HEAD_W)), per_seq((8, KV_LORA)),
                   per_seq((8, ROPE))]
                + [page((N_HEADS * PAGE, HEAD_W), i) for i in range(npg)]
                + [page((N_HEADS * PAGE, HEAD_W), i) for i in range(npg)]
                + [page((PAGE, KV_LORA), i) for i in range(npg)]
                + [page((PAGE, ROPE), i) for i in range(npg)])
    grid_spec = pltpu.PrefetchScalarGridSpec(
        num_scalar_prefetch=1,
        grid=(nb, n_pages // npg),
        in_specs=in_specs,
        out_specs=[per_seq((N_HEADS, n_tok, HEAD_W)), per_seq((rows_m, KV_LORA))],
        scratch_shapes=[pltpu.VMEM((N_HEADS, rows_d, 1), F32), pltpu.VMEM((N_HEADS, rows_d, 1), F32),
                        pltpu.VMEM((N_HEADS, rows_d, HEAD_W), F32),
                        pltpu.VMEM((rows_m, 1), F32), pltpu.VMEM((rows_m, 1), F32),
                        pltpu.VMEM((rows_m, KV_LORA), F32)])
    return pl.pallas_call(
        functools.partial(_sample_attn_kernel, n_tok=n_tok, pages_per_step=npg, past=past),
        grid_spec=grid_spec,
        out_shape=[jax.ShapeDtypeStruct((nb, N_HEADS, n_tok, HEAD_W), F32),
                   jax.ShapeDtypeStruct((nb, rows_m, KV_LORA), F32)],
        compiler_params=_params(("parallel", "arbitrary")),
        name="sample_attention",
    )(page_table, *lams, qd, qabs, qr, kn, vn, cn, rn,
      *([pool_dk] * npg), *([pool_dv] * npg), *([pool_ckv] * npg), *([pool_kr] * npg))


def _mem_kv_kernel(x_ref, g_ref, wk_ref, wv_ref, k_ref, v_ref):
    n = _rms(x_ref[...], g_ref[...]).astype(BF16)
    k_ref[...] = _dot(n, wk_ref[...])
    v_ref[...] = _dot(n, wv_ref[...])


def _mem_kv(x2, g, wk, wv):
    m = x2.shape[0]
    tm = _row_tile(m, 256)
    row = pl.BlockSpec((tm, D_MODEL), lambda i: (i, 0))
    return pl.pallas_call(
        _mem_kv_kernel,
        grid=(m // tm,),
        in_specs=[row, _full(g.shape), _full(wk.shape), _full(wv.shape)],
        out_specs=[row, row],
        out_shape=[jax.ShapeDtypeStruct((m, D_MODEL), F32)] * 2,
        compiler_params=_params(("parallel",)),
        name="mem_kv",
    )(x2, g, wk, wv)


def _mix(h, ya, yb_b16, g_subln, w_o):
    parts = []
    for hd in range(N_HEADS):
        sl = slice(hd * HEAD_W, (hd + 1) * HEAD_W)
        parts.append((_rms(ya[:, sl], g_subln) * (1.0 - LAM_INIT)).astype(BF16))
    o = jnp.concatenate(parts + [yb_b16], axis=1)
    return h + _dot(o, w_o)


def _softmax_rows(s):
    e = jnp.exp(s - jnp.max(s, axis=-1, keepdims=True))
    return e / jnp.sum(e, axis=-1, keepdims=True)


def _prompt_post_kernel(h_ref, ya_ref, yb_ref, gs_ref, wo_ref, gc_ref, wmq_ref, mk_ref, mv_ref, wmo_ref,
                        out_ref):
    h1 = _mix(h_ref[...], ya_ref[...], yb_ref[...].astype(BF16), gs_ref[...], wo_ref[...])
    q = _dot(_rms(h1, gc_ref[...]).astype(BF16), wmq_ref[...])
    outs = []
    for hd in range(N_HEADS):
        sl = slice(hd * MEM_HEAD, (hd + 1) * MEM_HEAD)
        p = _softmax_rows(_dot_nt(q[:, sl].astype(BF16), mk_ref[0, :, sl]) * MEM_SCALE)
        outs.append(_dot(p.astype(BF16), mv_ref[0, :, sl]).astype(BF16))
    out_ref[...] = h1 + _dot(jnp.concatenate(outs, axis=1), wmo_ref[...])


def _prompt_post(h2, ya2, yb2, g_subln, w_o, g_cross, w_mq, mk, mv, w_mo, seq):
    m = h2.shape[0]
    tm = _row_tile(seq, 512)
    per_seq = seq // tm
    row = lambda w: pl.BlockSpec((tm, w), lambda i: (i, 0))
    mem = pl.BlockSpec((1,) + mk.shape[1:], lambda i: (i // per_seq, 0, 0))
    return pl.pallas_call(
        _prompt_post_kernel,
        grid=(m // tm,),
        in_specs=[row(D_MODEL), row(HW), row(HW), _full(g_subln.shape), _full(w_o.shape), _full(g_cross.shape),
                  _full(w_mq.shape), mem, mem, _full(w_mo.shape)],
        out_specs=row(D_MODEL),
        out_shape=jax.ShapeDtypeStruct((m, D_MODEL), F32),
        compiler_params=_params(("parallel",)),
        name="prompt_post",
    )(h2, ya2, yb2, g_subln, w_o, g_cross, w_mq, mk, mv, w_mo)


def _sample_mix_kernel(h_ref, ya_ref, ol_ref, wuv_ref, gs_ref, wo_ref, gc_ref, wmq_ref, h1_ref, q_ref):
    yb = jnp.concatenate(
        [_dot(ol_ref[:, hd * KV_LORA:(hd + 1) * KV_LORA].astype(BF16), wuv_ref[hd]).astype(BF16)
         for hd in range(N_HEADS)], axis=1)
    h1 = _mix(h_ref[...], ya_ref[...], yb, gs_ref[...], wo_ref[...])
    h1_ref[...] = h1
    q_ref[...] = _dot(_rms(h1, gc_ref[...]).astype(BF16), wmq_ref[...])


def _sample_mix(h2, ya2, ol2, wuv_h, g_subln, w_o, g_cross, w_mq):
    m = h2.shape[0]
    tm = _row_tile(m, 256)
    row = lambda w: pl.BlockSpec((tm, w), lambda i: (i, 0))
    return pl.pallas_call(
        _sample_mix_kernel,
        grid=(m // tm,),
        in_specs=[row(D_MODEL), row(HW), row(N_HEADS * KV_LORA), _full(wuv_h.shape), _full(g_subln.shape),
                  _full(w_o.shape), _full(g_cross.shape), _full(w_mq.shape)],
        out_specs=[row(D_MODEL), row(D_MODEL)],
        out_shape=[jax.ShapeDtypeStruct((m, D_MODEL), F32)] * 2,
        compiler_params=_params(("parallel",)),
        name="sample_mix",
    )(h2, ya2, ol2, wuv_h, g_subln, w_o, g_cross, w_mq)


def _sample_cross_kernel(q_ref, mk_ref, mv_ref, o_ref, *, group):
    for g in range(group):
        for hd in range(N_HEADS):
            sl = slice(hd * MEM_HEAD, (hd + 1) * MEM_HEAD)
            p = _softmax_rows(_dot_nt(q_ref[g, :, sl].astype(BF16), mk_ref[g, :, sl].astype(BF16)) * MEM_SCALE)
            o_ref[g, :, sl] = _dot(p.astype(BF16), mv_ref[g, :, sl].astype(BF16))


def _sample_cross(q3, mk, mv):
    nb, n_tok, _ = q3.shape
    group = 8 if nb % 8 == 0 else 1
    qspec = pl.BlockSpec((group, n_tok, D_MODEL), lambda i: (i, 0, 0))
    mspec = pl.BlockSpec((group,) + mk.shape[1:], lambda i: (i, 0, 0))
    return pl.pallas_call(
        functools.partial(_sample_cross_kernel, group=group),
        grid=(nb // group,),
        in_specs=[qspec, mspec, mspec],
        out_specs=qspec,
        out_shape=jax.ShapeDtypeStruct(q3.shape, F32),
        compiler_params=_params(("parallel",)),
        name="sample_cross",
    )(q3, mk, mv)


def _ffn_tail(h2, g_ffn, w_gate_ref, w_up_ref, w_down_ref, g_final, chunks):
    n = _rms(h2, g_ffn).astype(BF16)
    d_ff = w_gate_ref.shape[1]
    step = d_ff // chunks
    acc = h2
    for j in range(chunks):
        sl = slice(j * step, (j + 1) * step)
        g = _dot(n, w_gate_ref[:, sl])
        a = (g * jax.nn.sigmoid(g)) * _dot(n, w_up_ref[:, sl])
        acc = acc + _dot(a.astype(BF16), w_down_ref[sl, :])
    return _rms(acc, g_final)


def _ffn_kernel(h_ref, gf_ref, wg_ref, wu_ref, wd_ref, gfin_ref, y_ref, *, chunks):
    y_ref[...] = _ffn_tail(h_ref[...], gf_ref[...], wg_ref, wu_ref, wd_ref, gfin_ref[...], chunks)


def _ffn_cross_kernel(h_ref, o_ref, wmo_ref, gf_ref, wg_ref, wu_ref, wd_ref, gfin_ref, y_ref, *, chunks):
    h2 = h_ref[...] + _dot(o_ref[...].astype(BF16), wmo_ref[...])
    y_ref[...] = _ffn_tail(h2, gf_ref[...], wg_ref, wu_ref, wd_ref, gfin_ref[...], chunks)


def _ffn(h2, g_ffn, w_gate, w_up, w_down, g_final, cross=None):
    m = h2.shape[0]
    tm = _row_tile(m, 256)
    d_ff = w_gate.shape[1]
    chunks = 2 if d_ff % (2 * HEAD_W) == 0 else 1
    row = pl.BlockSpec((tm, D_MODEL), lambda i: (i, 0))
    wspecs = [_full(g_ffn.shape), _full(w_gate.shape), _full(w_up.shape), _full(w_down.shape),
              _full(g_final.shape)]
    if cross is None:
        body, ins, specs = _ffn_kernel, (h2,), [row]
    else:
        o2, w_mo = cross
        body, ins, specs = _ffn_cross_kernel, (h2, o2, w_mo), [row, row, _full(w_mo.shape)]
    return pl.pallas_call(
        functools.partial(body, chunks=chunks),
        grid=(m // tm,),
        in_specs=specs + wspecs,
        out_specs=row,
        out_shape=jax.ShapeDtypeStruct((m, D_MODEL), F32),
        compiler_params=_params(("parallel",)),
        name="ffn" if cross is None else "ffn_cross",
    )(*ins, g_ffn, w_gate, w_up, w_down, g_final)


def kernel(x_prompt, x_sample, mem_prompt, cache_diff_k, cache_diff_v, cache_mla_ckv, cache_mla_krope, cache_mem_k, cache_mem_v, page_table, g_attn, w_in, lambda_q1, lambda_k1, lambda_q2, lambda_k2, g_subln, g_q_lat, w_uq, g_kv_lat, w_ukv, w_o, g_cross, g_mem, w_mq, w_mk, w_mv, w_mo, g_ffn, w_gate, w_up, w_down, g_final):
    depth = g_attn.shape[0]
    assert depth == 1, "single-layer step"
    bsz, seq, _ = x_prompt.shape
    nb, n_tok, _ = x_sample.shape
    n_pool = cache_diff_k.shape[1]
    past = page_table.shape[1] * PAGE
    mem_len = mem_prompt.shape[1]
    lyr = 0
    b16 = lambda w: w.astype(BF16)
    lams = (lambda_q1, lambda_k1, lambda_q2, lambda_k2)
    w_in_ext, wuq, wuq_sw, wuk, wuv, wabs, w_uv = _prep_proj_weights(w_in[lyr], w_uq[lyr], w_ukv[lyr])
    wuv_h = b16(jnp.transpose(w_uv, (1, 0, 2)))
    g_final2 = g_final.reshape(1, D_MODEL)
    w_o_b, w_mq_b, w_mo_b = b16(w_o[lyr]), b16(w_mq[lyr]), b16(w_mo[lyr])
    w_gate_b, w_up_b, w_down_b = b16(w_gate[lyr]), b16(w_up[lyr]), b16(w_down[lyr])

    xp = x_prompt.reshape(bsz * seq, D_MODEL)
    cs_p, sn_p = _rope_tables(jnp.arange(seq))
    (dk_p, dv_p, ckv_p, kr_p, qa, qb, kd, vd, qm, km, vm) = _proj_prompt(
        xp, cs_p, sn_p, g_attn, w_in_ext, g_q_lat, wuq, wuq_sw, g_kv_lat, wuk, wuv, seq)
    r3 = lambda a: a.reshape(bsz, seq, HW)
    ya_p, yb_p = _prompt_attention(lams, r3(qa), r3(qb), r3(qm), r3(kd), r3(vd), r3(km), r3(vm))
    mk_p, mv_p = _mem_kv(mem_prompt.reshape(bsz * mem_len, D_MODEL), g_mem, b16(w_mk[lyr]), b16(w_mv[lyr]))
    mem3 = lambda a: b16(a).reshape(bsz, mem_len, D_MODEL)
    h2_p = _prompt_post(xp, ya_p.reshape(bsz * seq, HW), yb_p.reshape(bsz * seq, HW), g_subln, w_o_b,
                        g_cross, w_mq_b, mem3(mk_p), mem3(mv_p), w_mo_b, seq)
    y_prompt = _ffn(h2_p, g_ffn, w_gate_b, w_up_b, w_down_b, g_final2).reshape(bsz, seq, D_MODEL)

    xs = x_sample.reshape(nb * n_tok, D_MODEL)
    cs_s, sn_s = _rope_tables(past + jnp.arange(n_tok))
    cs_s, sn_s = jnp.tile(cs_s, (nb, 1)), jnp.tile(sn_s, (nb, 1))
    (dk_s, dv_s, ckv_s, kr_s, dq_s, qrot_s, qabs_s, krw_s) = _proj_sample(
        xs, cs_s, sn_s, g_attn, w_in_ext, g_q_lat, wuq, wuq_sw, g_kv_lat, wabs)
    dq4 = dq_s.reshape(nb, n_tok, N_HEADS, 2, DIFF_QK).transpose(0, 2, 3, 1, 4)
    z = jnp.zeros_like(dq4[:, :, 0])
    qd = b16(jnp.concatenate([jnp.concatenate([dq4[:, :, 0], z], -1), jnp.concatenate([z, dq4[:, :, 1]], -1)], 2))
    to_ht = lambda a, w: a.reshape(nb, n_tok, N_HEADS, w).transpose(0, 2, 1, 3)
    qabs = b16(to_ht(qabs_s, KV_LORA).reshape(nb, N_HEADS * n_tok, KV_LORA))
    qr = b16(to_ht(qrot_s, HEAD_W)[..., NOPE:NOPE + ROPE].reshape(nb, N_HEADS * n_tok, ROPE))
    pad_tok = lambda a: jnp.pad(a, [(0, 0)] * (a.ndim - 2) + [(0, 8 - n_tok), (0, 0)])
    kn = b16(pad_tok(to_ht(dk_s, HEAD_W)))
    vn = b16(pad_tok(to_ht(dv_s, HEAD_W)))
    cn = b16(pad_tok(ckv_s.reshape(nb, n_tok, KV_LORA)))
    rn = b16(pad_tok(krw_s.reshape(nb, n_tok, HEAD_W)[..., NOPE:NOPE + ROPE]))
    ya_s, ol_s = _sample_attention(
        page_table, lams, qd, qabs, qr, kn, vn, cn, rn,
        cache_diff_k[lyr].reshape(n_pool, PAGE * N_HEADS, HEAD_W),
        cache_diff_v[lyr].reshape(n_pool, PAGE * N_HEADS, HEAD_W),
        cache_mla_ckv[lyr], cache_mla_krope[lyr], n_tok)
    ya_s2 = ya_s.transpose(0, 2, 1, 3).reshape(nb * n_tok, HW)
    ol_s2 = ol_s.reshape(nb, N_HEADS, n_tok, KV_LORA).transpose(0, 2, 1, 3).reshape(nb * n_tok, N_HEADS * KV_LORA)
    h1_s, q_s = _sample_mix(xs, ya_s2, ol_s2, wuv_h, g_subln, w_o_b, g_cross, w_mq_b)
    o_s = _sample_cross(q_s.reshape(nb, n_tok, D_MODEL),
                        cache_mem_k[lyr].reshape(nb, mem_len, D_MODEL),
                        cache_mem_v[lyr].reshape(nb, mem_len, D_MODEL))
    y_sample = _ffn(h1_s, g_ffn, w_gate_b, w_up_b, w_down_b, g_final2,
                    cross=(o_s.reshape(nb * n_tok, D_MODEL), w_mo_b)).reshape(nb, n_tok, D_MODEL)

    p5 = lambda a, w: a.reshape(depth, bsz, seq, N_HEADS, w)
    s5 = lambda a, w: a.reshape(depth, nb, n_tok, N_HEADS, w)
    return (y_prompt, y_sample,
            p5(dk_p, HEAD_W), p5(dv_p, HEAD_W),
            ckv_p.reshape(depth, bsz, seq, KV_LORA), kr_p.reshape(depth, bsz, seq, ROPE),
            mk_p.reshape(depth, bsz, mem_len, N_HEADS, MEM_HEAD), mv_p.reshape(depth, bsz, mem_len, N_HEADS, MEM_HEAD),
            s5(dk_s, HEAD_W), s5(dv_s, HEAD_W),
            ckv_s.reshape(depth, nb, n_tok, KV_LORA), kr_s.reshape(depth, nb, n_tok, ROPE))
```

```python
import functools
import math

import jax
import jax.numpy as jnp
import numpy as np
from jax import lax
from jax.experimental import pallas as pl
from jax.experimental.pallas import tpu as pltpu

F32 = jnp.float32
BF16 = jnp.bfloat16

D_MODEL = 1024
PAGE = 128
N_HEADS = 4
HEAD_W = 128
DIFF_QK = 64
Q_LORA = 384
KV_LORA = 256
NOPE = 64
ROPE = 32
MEM_HEAD = 256
ROPE_BASE = 10000.0
EPS = 1e-6
NEG_INF = -1e30
DIFF_SCALE = DIFF_QK ** -0.5
MLA_SCALE = (NOPE + ROPE) ** -0.5
MEM_SCALE = MEM_HEAD ** -0.5
LAM_INIT = 0.8 - 0.6 * math.exp(-0.3 * 0)
LOG2E = math.log2(math.e)
ALIBI = tuple(float(2.0 ** (-8.0 * (h + 1) / N_HEADS)) for h in range(N_HEADS))

HW = N_HEADS * HEAD_W
OFF_DQ, OFF_DK, OFF_DV, OFF_QLAT, OFF_CKV, OFF_KR, OFF_KRS, IN_EXT = (
    0, HW, 2 * HW, 3 * HW, 3 * HW + Q_LORA, 3 * HW + Q_LORA + KV_LORA,
    3 * HW + Q_LORA + KV_LORA + HEAD_W, 3 * HW + Q_LORA + KV_LORA + 2 * HEAD_W)

VMEM_LIMIT = 56 * 1024 * 1024
NT = (((1,), (1,)), ((), ()))


def _rms(x, g):
    return x * lax.rsqrt(jnp.mean(x * x, axis=-1, keepdims=True) + EPS) * g


def _dot(a, b):
    return jnp.dot(a, b, preferred_element_type=F32)


def _dot_nt(a, b):
    return lax.dot_general(a, b, NT, preferred_element_type=F32)


def _lam(lq1, lk1, lq2, lk2):
    return (jnp.exp(jnp.sum(lq1 * lk1, axis=-1, keepdims=True))
            - jnp.exp(jnp.sum(lq2 * lk2, axis=-1, keepdims=True)) + LAM_INIT)


def _full(shape):
    nd = len(shape)
    return pl.BlockSpec(shape, lambda *_: (0,) * nd)


def _params(sem):
    return pltpu.CompilerParams(dimension_semantics=sem, vmem_limit_bytes=VMEM_LIMIT)


def _proj_common(x_ref, cs_ref, sn_ref, g_attn_ref, w_in_ref, gq_ref, wuq_ref, wuqs_ref, gkv_ref):
    n = _rms(x_ref[...], g_attn_ref[...]).astype(BF16)

    def mm(lo, hi):
        return _dot(n, w_in_ref[:, lo:hi])

    cs, sn = cs_ref[...], sn_ref[...]
    nq = _rms(mm(OFF_QLAT, OFF_CKV), gq_ref[...]).astype(BF16)
    q = _dot(nq, wuq_ref[...])
    qs = _dot(nq, wuqs_ref[...])
    ckv = _rms(mm(OFF_CKV, OFF_KR), gkv_ref[...])
    kr = mm(OFF_KR, OFF_KRS) * cs + mm(OFF_KRS, IN_EXT) * sn
    return mm, cs, sn, q, qs, ckv, kr


def _proj_prompt_kernel(x_ref, cs_ref, sn_ref, g_attn_ref, w_in_ref, gq_ref, wuq_ref, wuqs_ref,
                        gkv_ref, wuk_ref, wuv_ref,
                        dk_ref, dv_ref, ckv_ref, kr_ref, qa_ref, qb_ref, kd_ref, vd_ref,
                        qm_ref, km_ref, vm_ref):
    mm, cs, sn, q, qs, ckv, kr = _proj_common(
        x_ref, cs_ref, sn_ref, g_attn_ref, w_in_ref, gq_ref, wuq_ref, wuqs_ref, gkv_ref)
    dq = mm(OFF_DQ, OFF_DK) * (DIFF_SCALE * LOG2E)
    first = (lax.broadcasted_iota(jnp.int32, dq.shape, 1) % HEAD_W) < DIFF_QK
    qa_ref[...] = jnp.where(first, dq, 0.0).astype(BF16)
    qb_ref[...] = jnp.where(first, 0.0, dq).astype(BF16)
    dk = mm(OFF_DK, OFF_DV)
    dk_ref[...] = dk
    kd_ref[...] = dk.astype(BF16)
    dv = mm(OFF_DV, OFF_QLAT)
    dv_ref[...] = dv
    vd_ref[...] = dv.astype(BF16)
    ckv_ref[...] = ckv
    kr_ref[...] = kr[:, NOPE:NOPE + ROPE]
    cb = ckv.astype(BF16)
    kn = _dot(cb, wuk_ref[...])
    vm_ref[...] = _dot(cb, wuv_ref[...]).astype(BF16)
    for h in range(N_HEADS):
        sl = slice(h * HEAD_W, (h + 1) * HEAD_W)
        qm_ref[:, sl] = ((q[:, sl] * cs + qs[:, sl] * sn) * (MLA_SCALE * LOG2E)).astype(BF16)
        km_ref[:, sl] = (kn[:, sl] + kr).astype(BF16)


def _proj_sample_kernel(x_ref, cs_ref, sn_ref, g_attn_ref, w_in_ref, gq_ref, wuq_ref, wuqs_ref,
                        gkv_ref, wabs_ref,
                        dk_ref, dv_ref, ckv_ref, kr_ref, dq_ref, qrot_ref, qabs_ref, krw_ref):
    mm, cs, sn, q, qs, ckv, kr = _proj_common(
        x_ref, cs_ref, sn_ref, g_attn_ref, w_in_ref, gq_ref, wuq_ref, wuqs_ref, gkv_ref)
    dq_ref[...] = mm(OFF_DQ, OFF_DK) * DIFF_SCALE
    dk_ref[...] = mm(OFF_DK, OFF_DV)
    dv_ref[...] = mm(OFF_DV, OFF_QLAT)
    ckv_ref[...] = ckv
    kr_ref[...] = kr[:, NOPE:NOPE + ROPE]
    krw_ref[...] = kr
    for h in range(N_HEADS):
        sl = slice(h * HEAD_W, (h + 1) * HEAD_W)
        qrot_ref[:, sl] = (q[:, sl] * cs + qs[:, sl] * sn) * MLA_SCALE
        qabs_ref[:, h * KV_LORA:(h + 1) * KV_LORA] = (
            _dot(q[:, sl].astype(BF16), wabs_ref[h]) * MLA_SCALE)


def _rope_tables(pos):
    half = ROPE // 2
    inv = ROPE_BASE ** (-jnp.arange(half, dtype=F32) / half)
    ang = pos.astype(F32)[:, None] * inv[None, :]
    cos, sin = jnp.cos(ang), jnp.sin(ang)
    n = pos.shape[0]
    cs = jnp.concatenate([jnp.ones((n, NOPE), F32), cos, cos, jnp.ones((n, HEAD_W - NOPE - ROPE), F32)], axis=1)
    sn = jnp.concatenate([jnp.zeros((n, NOPE), F32), -sin, sin, jnp.zeros((n, HEAD_W - NOPE - ROPE), F32)], axis=1)
    return cs, sn


def _prep_proj_weights(w_in, w_uq, w_ukv):
    half = ROPE // 2
    wkr = w_in[:, OFF_KR:OFF_KR + ROPE]
    pad_l = jnp.zeros((D_MODEL, NOPE), F32)
    pad_r = jnp.zeros((D_MODEL, HEAD_W - NOPE - ROPE), F32)
    wkr_sw = jnp.concatenate([wkr[:, half:], wkr[:, :half]], axis=1)
    w_in_ext = jnp.concatenate(
        [w_in[:, :OFF_KR], pad_l, wkr, pad_r, pad_l, wkr_sw, pad_r], axis=1).astype(BF16)
    zq = jnp.zeros((Q_LORA, N_HEADS, HEAD_W - NOPE - ROPE), F32)
    wuq = jnp.concatenate([w_uq, zq], axis=-1).reshape(Q_LORA, HW).astype(BF16)
    wuq_sw = jnp.concatenate(
        [jnp.zeros((Q_LORA, N_HEADS, NOPE), F32), w_uq[..., NOPE + half:], w_uq[..., NOPE:NOPE + half], zq],
        axis=-1).reshape(Q_LORA, HW).astype(BF16)
    w_uk = w_ukv[..., :NOPE]
    w_uv = w_ukv[..., NOPE:]
    wuk = jnp.concatenate([w_uk, jnp.zeros((KV_LORA, N_HEADS, HEAD_W - NOPE), F32)], axis=-1)
    wuk = wuk.reshape(KV_LORA, HW).astype(BF16)
    wuv = w_uv.reshape(KV_LORA, HW).astype(BF16)
    wabs = jnp.transpose(w_uk, (1, 2, 0))
    wabs = jnp.concatenate([wabs, jnp.zeros((N_HEADS, HEAD_W - NOPE, KV_LORA), F32)], axis=1).astype(BF16)
    return w_in_ext, wuq, wuq_sw, wuk, wuv, wabs, w_uv


def _row_tile(n, want):
    t = min(n, want)
    assert n % t == 0, (n, t)
    return t


def _proj_prompt(x2, cs, sn, g_attn, w_in_ext, gq, wuq, wuqs, gkv, wuk, wuv, seq):
    m = x2.shape[0]
    tm = _row_tile(seq, 512)
    per_seq = seq // tm
    row = lambda w: pl.BlockSpec((tm, w), lambda i: (i, 0))
    tab = pl.BlockSpec((tm, HEAD_W), lambda i: (i % per_seq, 0))
    f = lambda w: jax.ShapeDtypeStruct((m, w), F32)
    b = lambda w: jax.ShapeDtypeStruct((m, w), BF16)
    return pl.pallas_call(
        _proj_prompt_kernel,
        grid=(m // tm,),
        in_specs=[row(D_MODEL), tab, tab, _full(g_attn.shape), _full(w_in_ext.shape), _full(gq.shape),
                  _full(wuq.shape), _full(wuqs.shape), _full(gkv.shape), _full(wuk.shape), _full(wuv.shape)],
        out_specs=[row(HW), row(HW), row(KV_LORA), row(ROPE)] + [row(HW)] * 7,
        out_shape=[f(HW), f(HW), f(KV_LORA), f(ROPE)] + [b(HW)] * 7,
        compiler_params=_params(("parallel",)),
        name="proj_prompt",
    )(x2, cs, sn, g_attn, w_in_ext, gq, wuq, wuqs, gkv, wuk, wuv)


def _proj_sample(x2, cs, sn, g_attn, w_in_ext, gq, wuq, wuqs, gkv, wabs):
    m = x2.shape[0]
    tm = _row_tile(m, 256)
    row = lambda w: pl.BlockSpec((tm, w), lambda i: (i, 0))
    f = lambda w: jax.ShapeDtypeStruct((m, w), F32)
    return pl.pallas_call(
        _proj_sample_kernel,
        grid=(m // tm,),
        in_specs=[row(D_MODEL), row(HEAD_W), row(HEAD_W), _full(g_attn.shape), _full(w_in_ext.shape),
                  _full(gq.shape), _full(wuq.shape), _full(wuqs.shape), _full(gkv.shape), _full(wabs.shape)],
        out_specs=[row(HW), row(HW), row(KV_LORA), row(ROPE), row(HW), row(HW), row(N_HEADS * KV_LORA),
                   row(HEAD_W)],
        out_shape=[f(HW), f(HW), f(KV_LORA), f(ROPE), f(HW), f(HW), f(N_HEADS * KV_LORA), f(HEAD_W)],
        compiler_params=_params(("parallel",)),
        name="proj_sample",
    )(x2, cs, sn, g_attn, w_in_ext, gq, wuq, wuqs, gkv, wabs)


def _online_update(s, m_ref, l_ref, acc_ref, idx, v):
    m_prev = m_ref[idx]
    m_new = jnp.maximum(m_prev, jnp.max(s, axis=-1, keepdims=True))
    alpha = jnp.exp2(m_prev - m_new)
    p = jnp.exp2(s - m_new)
    l_ref[idx] = alpha * l_ref[idx] + jnp.sum(p, axis=-1, keepdims=True)
    acc_ref[idx] = alpha * acc_ref[idx] + _dot(p.astype(BF16), v)
    m_ref[idx] = m_new


def _prompt_attn_kernel(lq1_ref, lk1_ref, lq2_ref, lk2_ref, qa_ref, qb_ref, qm_ref,
                        kd_ref, vd_ref, km_ref, vm_ref, ya_ref, yb_ref, m_ref, l_ref, acc_ref, *, tq, tk):
    qi, ki = pl.program_id(1), pl.program_id(2)

    @pl.when(ki == 0)
    def _():
        m_ref[...] = jnp.full(m_ref.shape, NEG_INF, F32)
        l_ref[...] = jnp.zeros(l_ref.shape, F32)
        acc_ref[...] = jnp.zeros(acc_ref.shape, F32)

    def step(masked):
        rel = (qi * tq - ki * tk
               + lax.broadcasted_iota(jnp.int32, (tq, tk), 0) - lax.broadcasted_iota(jnp.int32, (tq, tk), 1))
        dist = rel.astype(F32)
        keep = (lambda s: jnp.where(rel >= 0, s, NEG_INF)) if masked else (lambda s: s)
        for h in range(N_HEADS):
            sl = slice(h * HEAD_W, (h + 1) * HEAD_W)
            k = kd_ref[0, :, sl]
            v = vd_ref[0, :, sl]
            bias = (ALIBI[h] * LOG2E) * dist
            for c, q_ref in enumerate((qa_ref, qb_ref)):
                s = _dot_nt(q_ref[0, :, sl], k) - bias
                _online_update(keep(s), m_ref, l_ref, acc_ref, 2 * h + c, v)
            s = _dot_nt(qm_ref[0, :, sl], km_ref[0, :, sl])
            _online_update(keep(s), m_ref, l_ref, acc_ref, 2 * N_HEADS + h, vm_ref[0, :, sl])

    below_diagonal = (ki + 1) * tk - 1 <= qi * tq
    pl.when(below_diagonal)(lambda: step(False))
    pl.when(jnp.logical_not(below_diagonal) & (ki * tk < (qi + 1) * tq))(lambda: step(True))

    @pl.when(((ki + 1) * tk >= (qi + 1) * tq) & (ki * tk < (qi + 1) * tq))
    def _():
        lam = _lam(lq1_ref[...], lk1_ref[...], lq2_ref[...], lk2_ref[...])
        for h in range(N_HEADS):
            sl = slice(h * HEAD_W, (h + 1) * HEAD_W)
            ya_ref[0, :, sl] = (acc_ref[2 * h] / l_ref[2 * h]
                                - lam * (acc_ref[2 * h + 1] / l_ref[2 * h + 1]))
            yb_ref[0, :, sl] = acc_ref[2 * N_HEADS + h] / l_ref[2 * N_HEADS + h]


def _prompt_attention(lams, qa, qb, qm, kd, vd, km, vm):
    bsz, seq, _ = qa.shape
    tq = _row_tile(seq, 512)
    tk = tq
    qspec = pl.BlockSpec((1, tq, HW), lambda b, i, j: (b, i, 0))
    kspec = pl.BlockSpec((1, tk, HW), lambda b, i, j: (b, jnp.minimum(j, ((i + 1) * tq - 1) // tk), 0))
    n_state = 3 * N_HEADS
    return pl.pallas_call(
        functools.partial(_prompt_attn_kernel, tq=tq, tk=tk),
        grid=(bsz, seq // tq, seq // tk),
        in_specs=[_full(l.shape) for l in lams] + [qspec] * 3 + [kspec] * 4,
        out_specs=[qspec, qspec],
        out_shape=[jax.ShapeDtypeStruct((bsz, seq, HW), F32)] * 2,
        scratch_shapes=[pltpu.VMEM((n_state, tq, 1), F32), pltpu.VMEM((n_state, tq, 1), F32),
                        pltpu.VMEM((n_state, tq, HEAD_W), F32)],
        compiler_params=_params(("parallel", "parallel", "arbitrary")),
        name="prompt_attention",
    )(*lams, qa, qb, qm, kd, vd, km, vm)


def _sample_attn_kernel(pt_ref, lq1_ref, lk1_ref, lq2_ref, lk2_ref, qd_ref, qabs_ref, qr_ref,
                        kn_ref, vn_ref, cn_ref, rn_ref, *rest, n_tok, pages_per_step, sub_pages, past):
    del pt_ref
    npg = pages_per_step
    dk_refs, dv_refs = rest[:npg], rest[npg:2 * npg]
    ck_refs, kr_refs = rest[2 * npg:3 * npg], rest[3 * npg:4 * npg]
    ya_ref, ol_ref, m_ref, l_ref, ad_ref, am_ref = rest[4 * npg:]
    c = pl.program_id(1)
    rows_h = 2 * n_tok
    rows_d = N_HEADS * rows_h
    rows_m = N_HEADS * n_tok
    head_rows = [slice(h * rows_h, (h + 1) * rows_h) for h in range(N_HEADS)]

    @pl.when(c == 0)
    def _():
        def tok_bias(rows):
            t = lax.broadcasted_iota(jnp.int32, (rows, 8), 0) % n_tok
            j = lax.broadcasted_iota(jnp.int32, (rows, 8), 1)
            return (j <= t) & (j < n_tok), (t - j).astype(F32)

        mask, dist = tok_bias(rows_h)
        s_d = [jnp.where(mask, _dot_nt(qd_ref[0, h], kn_ref[0, h]) - ALIBI[h] * dist, NEG_INF)
               for h in range(N_HEADS)]
        mask, _ = tok_bias(rows_m)
        cn = cn_ref[0]
        s_m = jnp.where(mask, _dot_nt(qabs_ref[0], cn) + _dot_nt(qr_ref[0], rn_ref[0]), NEG_INF)
        s = jnp.concatenate(s_d + [s_m], axis=0)
        m = jnp.max(s, axis=-1, keepdims=True)
        p = jnp.exp(s - m)
        m_ref[...] = m
        l_ref[...] = jnp.sum(p, axis=-1, keepdims=True)
        for h in range(N_HEADS):
            ad_ref[head_rows[h]] = _dot(p[head_rows[h]].astype(BF16), vn_ref[0, h])
        am_ref[...] = _dot(p[rows_d:].astype(BF16), cn)

    width = sub_pages * PAGE
    row = lax.broadcasted_iota(jnp.int32, (rows_d, 1), 0)
    slope = jnp.full((rows_d, 1), ALIBI[N_HEADS - 1], F32)
    for h in range(N_HEADS - 2, -1, -1):
        slope = jnp.where(row < (h + 1) * rows_h, ALIBI[h], slope)
    qpos = past + lax.broadcasted_iota(jnp.int32, (rows_d, width), 0) % n_tok
    col = lax.broadcasted_iota(jnp.int32, (rows_d, width), 1)
    qabs, qr = qabs_ref[0], qr_ref[0]
    for g in range(npg // sub_pages):
        pages = range(g * sub_pages, (g + 1) * sub_pages)
        bias = slope * (qpos - ((c * npg + g * sub_pages) * PAGE + col)).astype(F32)
        s_d = jnp.concatenate(
            [jnp.concatenate(
                [_dot_nt(qd_ref[0, h], dk_refs[i][0, pl.ds(h, PAGE, stride=N_HEADS), :].astype(BF16))
                 for i in pages], axis=1) for h in range(N_HEADS)], axis=0)
        cks = [ck_refs[i][0].astype(BF16) for i in pages]
        s_m = jnp.concatenate(
            [_dot_nt(qabs, ck) + _dot(qr, kr_refs[i][0].astype(BF16)) for i, ck in zip(pages, cks)], axis=1)
        s = jnp.concatenate([s_d - bias, s_m], axis=0)
        m_prev = m_ref[...]
        m_new = jnp.maximum(m_prev, jnp.max(s, axis=-1, keepdims=True))
        alpha = jnp.exp(m_prev - m_new)
        p = jnp.exp(s - m_new)
        l_ref[...] = alpha * l_ref[...] + jnp.sum(p, axis=-1, keepdims=True)
        m_ref[...] = m_new
        for h in range(N_HEADS):
            pb = p[head_rows[h]].astype(BF16)
            acc = alpha[head_rows[h]] * ad_ref[head_rows[h]]
            for j, i in enumerate(pages):
                acc += _dot(pb[:, j * PAGE:(j + 1) * PAGE],
                            dv_refs[i][0, pl.ds(h, PAGE, stride=N_HEADS), :].astype(BF16))
            ad_ref[head_rows[h]] = acc
        pb = p[rows_d:].astype(BF16)
        acc = alpha[rows_d:] * am_ref[...]
        for j, ck in enumerate(cks):
            acc += _dot(pb[:, j * PAGE:(j + 1) * PAGE], ck)
        am_ref[...] = acc

    @pl.when(c == pl.num_programs(1) - 1)
    def _():
        lam = _lam(lq1_ref[...], lk1_ref[...], lq2_ref[...], lk2_ref[...])
        o = ad_ref[...] / l_ref[:rows_d]
        for h in range(N_HEADS):
            ya_ref[0, h] = o[h * rows_h:h * rows_h + n_tok] - lam * o[h * rows_h + n_tok:(h + 1) * rows_h]
        ol_ref[0] = am_ref[...] / l_ref[rows_d:]


def _sample_attention(page_table, lams, qd, qabs, qr, kn, vn, cn, rn, pool_dk, pool_dv, pool_ckv, pool_kr,
                      n_tok):
    nb, n_pages = page_table.shape
    npg = next(p for p in (16, 8, 4, 2, 1) if n_pages % p == 0)
    sub = min(npg, 8)
    past = n_pages * PAGE
    rows_d, rows_m = 2 * n_tok, N_HEADS * n_tok
    rows_all = N_HEADS * rows_d + rows_m

    def per_seq(shape):
        nd = len(shape)
        return pl.BlockSpec((1,) + shape, lambda b, c, pt: (b,) + (0,) * nd)

    def page(shape, i):
        return pl.BlockSpec((1,) + shape, lambda b, c, pt: (pt[b, c * npg + i], 0, 0))

    in_specs = ([pl.BlockSpec(l.shape, lambda b, c, pt: (0, 0)) for l in lams]
                + [per_seq((N_HEADS, rows_d, HEAD_W)), per_seq((rows_m, KV_LORA)), per_seq((rows_m, ROPE)),
                   per_seq((N_HEADS, 8, HEAD_W)), per_seq((N_HEADS, 8, HEAD_W)), per_seq((8, KV_LORA)),
                   per_seq((8, ROPE))]
                + [page((N_HEADS * PAGE, HEAD_W), i) for i in range(npg)]
                + [page((N_HEADS * PAGE, HEAD_W), i) for i in range(npg)]
                + [page((PAGE, KV_LORA), i) for i in range(npg)]
                + [page((ROPE, PAGE), i) for i in range(npg)])
    grid_spec = pltpu.PrefetchScalarGridSpec(
        num_scalar_prefetch=1,
        grid=(nb, n_pages // npg),
        in_specs=in_specs,
        out_specs=[per_seq((N_HEADS, n_tok, HEAD_W)), per_seq((rows_m, KV_LORA))],
        scratch_shapes=[pltpu.VMEM((rows_all, 1), F32), pltpu.VMEM((rows_all, 1), F32),
                        pltpu.VMEM((N_HEADS * rows_d, HEAD_W), F32), pltpu.VMEM((rows_m, KV_LORA), F32)])
    return pl.pallas_call(
        functools.partial(_sample_attn_kernel, n_tok=n_tok, pages_per_step=npg, sub_pages=sub, past=past),
        grid_spec=grid_spec,
        out_shape=[jax.ShapeDtypeStruct((nb, N_HEADS, n_tok, HEAD_W), F32),
                   jax.ShapeDtypeStruct((nb, rows_m, KV_LORA), F32)],
        compiler_params=_params(("parallel", "arbitrary")),
        name="sample_attention",
    )(page_table, *lams, qd, qabs, qr, kn, vn, cn, rn,
      *([pool_dk] * npg), *([pool_dv] * npg), *([pool_ckv] * npg), *([pool_kr] * npg))


def _mem_kv_kernel(x_ref, g_ref, wk_ref, wv_ref, k_ref, v_ref):
    n = _rms(x_ref[...], g_ref[...]).astype(BF16)
    k_ref[...] = _dot(n, wk_ref[...])
    v_ref[...] = _dot(n, wv_ref[...])


def _mem_kv(x2, g, wk, wv):
    m = x2.shape[0]
    tm = _row_tile(m, 256)
    row = pl.BlockSpec((tm, D_MODEL), lambda i: (i, 0))
    return pl.pallas_call(
        _mem_kv_kernel,
        grid=(m // tm,),
        in_specs=[row, _full(g.shape), _full(wk.shape), _full(wv.shape)],
        out_specs=[row, row],
        out_shape=[jax.ShapeDtypeStruct((m, D_MODEL), F32)] * 2,
        compiler_params=_params(("parallel",)),
        name="mem_kv",
    )(x2, g, wk, wv)


def _mix(h, ya, yb_b16, g_subln, w_o):
    parts = []
    for hd in range(N_HEADS):
        sl = slice(hd * HEAD_W, (hd + 1) * HEAD_W)
        parts.append((_rms(ya[:, sl], g_subln) * (1.0 - LAM_INIT)).astype(BF16))
    o = jnp.concatenate(parts + [yb_b16], axis=1)
    return h + _dot(o, w_o)


def _softmax_rows(s):
    e = jnp.exp(s - jnp.max(s, axis=-1, keepdims=True))
    return e / jnp.sum(e, axis=-1, keepdims=True)


def _prompt_post_kernel(h_ref, ya_ref, yb_ref, gs_ref, wo_ref, gc_ref, wmq_ref, mk_ref, mv_ref, wmo_ref,
                        out_ref):
    h1 = _mix(h_ref[...], ya_ref[...], yb_ref[...].astype(BF16), gs_ref[...], wo_ref[...])
    q = _dot(_rms(h1, gc_ref[...]).astype(BF16), wmq_ref[...])
    outs = []
    for hd in range(N_HEADS):
        sl = slice(hd * MEM_HEAD, (hd + 1) * MEM_HEAD)
        p = _softmax_rows(_dot_nt(q[:, sl].astype(BF16), mk_ref[0, :, sl]) * MEM_SCALE)
        outs.append(_dot(p.astype(BF16), mv_ref[0, :, sl]).astype(BF16))
    out_ref[...] = h1 + _dot(jnp.concatenate(outs, axis=1), wmo_ref[...])


def _prompt_post(h2, ya2, yb2, g_subln, w_o, g_cross, w_mq, mk, mv, w_mo, seq):
    m = h2.shape[0]
    tm = _row_tile(seq, 512)
    per_seq = seq // tm
    row = lambda w: pl.BlockSpec((tm, w), lambda i: (i, 0))
    mem = pl.BlockSpec((1,) + mk.shape[1:], lambda i: (i // per_seq, 0, 0))
    return pl.pallas_call(
        _prompt_post_kernel,
        grid=(m // tm,),
        in_specs=[row(D_MODEL), row(HW), row(HW), _full(g_subln.shape), _full(w_o.shape), _full(g_cross.shape),
                  _full(w_mq.shape), mem, mem, _full(w_mo.shape)],
        out_specs=row(D_MODEL),
        out_shape=jax.ShapeDtypeStruct((m, D_MODEL), F32),
        compiler_params=_params(("parallel",)),
        name="prompt_post",
    )(h2, ya2, yb2, g_subln, w_o, g_cross, w_mq, mk, mv, w_mo)


def _sample_mix_kernel(h_ref, ya_ref, ol_ref, wuv_ref, gs_ref, wo_ref, gc_ref, wmq_ref, h1_ref, q_ref):
    yb = jnp.concatenate(
        [_dot(ol_ref[:, hd * KV_LORA:(hd + 1) * KV_LORA].astype(BF16), wuv_ref[hd]).astype(BF16)
         for hd in range(N_HEADS)], axis=1)
    h1 = _mix(h_ref[...], ya_ref[...], yb, gs_ref[...], wo_ref[...])
    h1_ref[...] = h1
    q_ref[...] = _dot(_rms(h1, gc_ref[...]).astype(BF16), wmq_ref[...])


def _sample_mix(h2, ya2, ol2, wuv_h, g_subln, w_o, g_cross, w_mq):
    m = h2.shape[0]
    tm = _row_tile(m, 256)
    row = lambda w: pl.BlockSpec((tm, w), lambda i: (i, 0))
    return pl.pallas_call(
        _sample_mix_kernel,
        grid=(m // tm,),
        in_specs=[row(D_MODEL), row(HW), row(N_HEADS * KV_LORA), _full(wuv_h.shape), _full(g_subln.shape),
                  _full(w_o.shape), _full(g_cross.shape), _full(w_mq.shape)],
        out_specs=[row(D_MODEL), row(D_MODEL)],
        out_shape=[jax.ShapeDtypeStruct((m, D_MODEL), F32)] * 2,
        compiler_params=_params(("parallel",)),
        name="sample_mix",
    )(h2, ya2, ol2, wuv_h, g_subln, w_o, g_cross, w_mq)


def _sample_cross_kernel(q_ref, mk_ref, mv_ref, o_ref, *, group):
    for g in range(group):
        for hd in range(N_HEADS):
            sl = slice(hd * MEM_HEAD, (hd + 1) * MEM_HEAD)
            p = _softmax_rows(_dot_nt(q_ref[g, :, sl].astype(BF16), mk_ref[g, :, sl].astype(BF16)) * MEM_SCALE)
            o_ref[g, :, sl] = _dot(p.astype(BF16), mv_ref[g, :, sl].astype(BF16))


def _sample_cross(q3, mk, mv):
    nb, n_tok, _ = q3.shape
    group = 8 if nb % 8 == 0 else 1
    qspec = pl.BlockSpec((group, n_tok, D_MODEL), lambda i: (i, 0, 0))
    mspec = pl.BlockSpec((group,) + mk.shape[1:], lambda i: (i, 0, 0))
    return pl.pallas_call(
        functools.partial(_sample_cross_kernel, group=group),
        grid=(nb // group,),
        in_specs=[qspec, mspec, mspec],
        out_specs=qspec,
        out_shape=jax.ShapeDtypeStruct(q3.shape, F32),
        compiler_params=_params(("parallel",)),
        name="sample_cross",
    )(q3, mk, mv)


def _ffn_tail(h2, g_ffn, w_gate_ref, w_up_ref, w_down_ref, g_final, chunks):
    n = _rms(h2, g_ffn).astype(BF16)
    d_ff = w_gate_ref.shape[1]
    step = d_ff // chunks
    acc = h2
    for j in range(chunks):
        sl = slice(j * step, (j + 1) * step)
        g = _dot(n, w_gate_ref[:, sl])
        a = (g * jax.nn.sigmoid(g)) * _dot(n, w_up_ref[:, sl])
        acc = acc + _dot(a.astype(BF16), w_down_ref[sl, :])
    return _rms(acc, g_final)


def _ffn_kernel(h_ref, gf_ref, wg_ref, wu_ref, wd_ref, gfin_ref, y_ref, *, chunks):
    y_ref[...] = _ffn_tail(h_ref[...], gf_ref[...], wg_ref, wu_ref, wd_ref, gfin_ref[...], chunks)


def _ffn_cross_kernel(h_ref, o_ref, wmo_ref, gf_ref, wg_ref, wu_ref, wd_ref, gfin_ref, y_ref, *, chunks):
    h2 = h_ref[...] + _dot(o_ref[...].astype(BF16), wmo_ref[...])
    y_ref[...] = _ffn_tail(h2, gf_ref[...], wg_ref, wu_ref, wd_ref, gfin_ref[...], chunks)


def _ffn(h2, g_ffn, w_gate, w_up, w_down, g_final, cross=None):
    m = h2.shape[0]
    tm = _row_tile(m, 256)
    d_ff = w_gate.shape[1]
    chunks = 2 if d_ff % (2 * HEAD_W) == 0 else 1
    row = pl.BlockSpec((tm, D_MODEL), lambda i: (i, 0))
    wspecs = [_full(g_ffn.shape), _full(w_gate.shape), _full(w_up.shape), _full(w_down.shape),
              _full(g_final.shape)]
    if cross is None:
        body, ins, specs = _ffn_kernel, (h2,), [row]
    else:
        o2, w_mo = cross
        body, ins, specs = _ffn_cross_kernel, (h2, o2, w_mo), [row, row, _full(w_mo.shape)]
    return pl.pallas_call(
        functools.partial(body, chunks=chunks),
        grid=(m // tm,),
        in_specs=specs + wspecs,
        out_specs=row,
        out_shape=jax.ShapeDtypeStruct((m, D_MODEL), F32),
        compiler_params=_params(("parallel",)),
        name="ffn" if cross is None else "ffn_cross",
    )(*ins, g_ffn, w_gate, w_up, w_down, g_final)


def kernel(x_prompt, x_sample, mem_prompt, cache_diff_k, cache_diff_v, cache_mla_ckv, cache_mla_krope, cache_mem_k, cache_mem_v, page_table, g_attn, w_in, lambda_q1, lambda_k1, lambda_q2, lambda_k2, g_subln, g_q_lat, w_uq, g_kv_lat, w_ukv, w_o, g_cross, g_mem, w_mq, w_mk, w_mv, w_mo, g_ffn, w_gate, w_up, w_down, g_final):
    depth = g_attn.shape[0]
    assert depth == 1, "single-layer step"
    bsz, seq, _ = x_prompt.shape
    nb, n_tok, _ = x_sample.shape
    n_pool = cache_diff_k.shape[1]
    past = page_table.shape[1] * PAGE
    mem_len = mem_prompt.shape[1]
    lyr = 0
    b16 = lambda w: w.astype(BF16)
    lams = (lambda_q1, lambda_k1, lambda_q2, lambda_k2)
    w_in_ext, wuq, wuq_sw, wuk, wuv, wabs, w_uv = _prep_proj_weights(w_in[lyr], w_uq[lyr], w_ukv[lyr])
    wuv_h = b16(jnp.transpose(w_uv, (1, 0, 2)))
    g_final2 = g_final.reshape(1, D_MODEL)
    w_o_b, w_mq_b, w_mo_b = b16(w_o[lyr]), b16(w_mq[lyr]), b16(w_mo[lyr])
    w_gate_b, w_up_b, w_down_b = b16(w_gate[lyr]), b16(w_up[lyr]), b16(w_down[lyr])

    xp = x_prompt.reshape(bsz * seq, D_MODEL)
    cs_p, sn_p = _rope_tables(jnp.arange(seq))
    (dk_p, dv_p, ckv_p, kr_p, qa, qb, kd, vd, qm, km, vm) = _proj_prompt(
        xp, cs_p, sn_p, g_attn, w_in_ext, g_q_lat, wuq, wuq_sw, g_kv_lat, wuk, wuv, seq)
    r3 = lambda a: a.reshape(bsz, seq, HW)
    ya_p, yb_p = _prompt_attention(lams, r3(qa), r3(qb), r3(qm), r3(kd), r3(vd), r3(km), r3(vm))
    mk_p, mv_p = _mem_kv(mem_prompt.reshape(bsz * mem_len, D_MODEL), g_mem, b16(w_mk[lyr]), b16(w_mv[lyr]))
    mem3 = lambda a: b16(a).reshape(bsz, mem_len, D_MODEL)
    h2_p = _prompt_post(xp, ya_p.reshape(bsz * seq, HW), yb_p.reshape(bsz * seq, HW), g_subln, w_o_b,
                        g_cross, w_mq_b, mem3(mk_p), mem3(mv_p), w_mo_b, seq)
    y_prompt = _ffn(h2_p, g_ffn, w_gate_b, w_up_b, w_down_b, g_final2).reshape(bsz, seq, D_MODEL)

    xs = x_sample.reshape(nb * n_tok, D_MODEL)
    cs_s, sn_s = _rope_tables(past + jnp.arange(n_tok))
    cs_s, sn_s = jnp.tile(cs_s, (nb, 1)), jnp.tile(sn_s, (nb, 1))
    (dk_s, dv_s, ckv_s, kr_s, dq_s, qrot_s, qabs_s, krw_s) = _proj_sample(
        xs, cs_s, sn_s, g_attn, w_in_ext, g_q_lat, wuq, wuq_sw, g_kv_lat, wabs)
    dq4 = dq_s.reshape(nb, n_tok, N_HEADS, 2, DIFF_QK).transpose(0, 2, 3, 1, 4)
    z = jnp.zeros_like(dq4[:, :, 0])
    qd = b16(jnp.concatenate([jnp.concatenate([dq4[:, :, 0], z], -1), jnp.concatenate([z, dq4[:, :, 1]], -1)], 2))
    to_ht = lambda a, w: a.reshape(nb, n_tok, N_HEADS, w).transpose(0, 2, 1, 3)
    qabs = b16(to_ht(qabs_s, KV_LORA).reshape(nb, N_HEADS * n_tok, KV_LORA))
    qr = b16(to_ht(qrot_s, HEAD_W)[..., NOPE:NOPE + ROPE].reshape(nb, N_HEADS * n_tok, ROPE))
    pad_tok = lambda a: jnp.pad(a, [(0, 0)] * (a.ndim - 2) + [(0, 8 - n_tok), (0, 0)])
    kn = b16(pad_tok(to_ht(dk_s, HEAD_W)))
    vn = b16(pad_tok(to_ht(dv_s, HEAD_W)))
    cn = b16(pad_tok(ckv_s.reshape(nb, n_tok, KV_LORA)))
    rn = b16(pad_tok(krw_s.reshape(nb, n_tok, HEAD_W)[..., NOPE:NOPE + ROPE]))
    ya_s, ol_s = _sample_attention(
        page_table, lams, qd, qabs, qr, kn, vn, cn, rn,
        cache_diff_k[lyr].reshape(n_pool, PAGE * N_HEADS, HEAD_W),
        cache_diff_v[lyr].reshape(n_pool, PAGE * N_HEADS, HEAD_W),
        cache_mla_ckv[lyr], jnp.swapaxes(cache_mla_krope[lyr], 1, 2), n_tok)
    ya_s2 = ya_s.transpose(0, 2, 1, 3).reshape(nb * n_tok, HW)
    ol_s2 = ol_s.reshape(nb, N_HEADS, n_tok, KV_LORA).transpose(0, 2, 1, 3).reshape(nb * n_tok, N_HEADS * KV_LORA)
    h1_s, q_s = _sample_mix(xs, ya_s2, ol_s2, wuv_h, g_subln, w_o_b, g_cross, w_mq_b)
    o_s = _sample_cross(q_s.reshape(nb, n_tok, D_MODEL),
                        cache_mem_k[lyr].reshape(nb, mem_len, D_MODEL),
                        cache_mem_v[lyr].reshape(nb, mem_len, D_MODEL))
    y_sample = _ffn(h1_s, g_ffn, w_gate_b, w_up_b, w_down_b, g_final2,
                    cross=(o_s.reshape(nb * n_tok, D_MODEL), w_mo_b)).reshape(nb, n_tok, D_MODEL)

    p5 = lambda a, w: a.reshape(depth, bsz, seq, N_HEADS, w)
    s5 = lambda a, w: a.reshape(depth, nb, n_tok, N_HEADS, w)
    return (y_prompt, y_sample,
            p5(dk_p, HEAD_W), p5(dv_p, HEAD_W),
            ckv_p.reshape(depth, bsz, seq, KV_LORA), kr_p.reshape(depth, bsz, seq, ROPE),
            mk_p.reshape(depth, bsz, mem_len, N_HEADS, MEM_HEAD), mv_p.reshape(depth, bsz, mem_len, N_HEADS, MEM_HEAD),
            s5(dk_s, HEAD_W), s5(dv_s, HEAD_W),
            ckv_s.reshape(depth, nb, n_tok, KV_LORA), kr_s.reshape(depth, nb, n_tok, ROPE))
```
